```python
import jax
import jax.numpy as jnp
from jax import lax
import numpy as np

D_MODEL = 1024
BATCH = 8
SEQ = 4096
DEPTH = 2

GRID_W = 64
CTX_LEN = 256
EPS = 1e-6
CONV_DIM = 512
CONV_K = 31
MLSTM_HEADS = 4
MLSTM_DIM = 1024
MLSTM_HEAD_DIM = MLSTM_DIM // MLSTM_HEADS
MLSTM_CHUNK = 64
M_INIT = -1e30
ATTN_HEADS = 8
ATTN_KV_HEADS = 2
ATTN_GROUP = ATTN_HEADS // ATTN_KV_HEADS
ATTN_HEAD_DIM = 64
ATTN_DIM = ATTN_HEADS * ATTN_HEAD_DIM
KV_DIM = ATTN_KV_HEADS * ATTN_HEAD_DIM
Q_BLOCK = 128
ROPE_THETA = 10000.0
D_FF = 2816
FFN_CONV_K = 3
N_BRANCH = 3
IN_SIZES = (2 * CONV_DIM, 3 * MLSTM_DIM, MLSTM_DIM, 4 * MLSTM_HEADS, ATTN_DIM, KV_DIM, KV_DIM, N_BRANCH * D_MODEL)
N_IN = sum(IN_SIZES)

kernel_name = "hybrid_conv_mlstm_gqa_prefix_dit_block"


def rmsnorm(x, w):
    xf = x.astype(jnp.float32)
    y = xf * lax.rsqrt(jnp.mean(xf * xf, axis=-1, keepdims=True) + EPS)
    return (y * w.astype(jnp.float32)).astype(x.dtype)


def layernorm(x, w, b):
    xf = x.astype(jnp.float32)
    xc = xf - jnp.mean(xf, axis=-1, keepdims=True)
    y = xc * lax.rsqrt(jnp.mean(xc * xc, axis=-1, keepdims=True) + EPS)
    return (y * w.astype(jnp.float32) + b.astype(jnp.float32)).astype(x.dtype)


def dwconv(x, w, b):
    k = w.shape[0]
    y = lax.conv_general_dilated(x, w[:, None, :].astype(x.dtype), window_strides=(1,),
                                 padding=((k // 2, k // 2),), dimension_numbers=("NWC", "WIO", "NWC"),
                                 feature_group_count=x.shape[-1])
    return y + b.astype(x.dtype)


def axial_rope_tables(rows):
    row = jnp.broadcast_to(jnp.arange(rows, dtype=jnp.float32)[:, None], (rows, GRID_W)).reshape(-1)
    col = jnp.broadcast_to(jnp.arange(GRID_W, dtype=jnp.float32)[None, :], (rows, GRID_W)).reshape(-1)
    n_freq = ATTN_HEAD_DIM // 4
    inv_freq = ROPE_THETA ** (-jnp.arange(n_freq, dtype=jnp.float32) / n_freq)
    ang = jnp.concatenate([row[:, None] * inv_freq, col[:, None] * inv_freq], axis=-1)
    return jnp.cos(ang), jnp.sin(ang)


def apply_rope(x, cos, sin):
    half = x.shape[-1] // 2
    x1, x2 = x[..., :half], x[..., half:]
    c = cos[None, :, None, :]
    s = sin[None, :, None, :]
    return jnp.concatenate([x1 * c - x2 * s, x1 * s + x2 * c], axis=-1).astype(x.dtype)


def mlstm_chunked(q, k, v, log_i, log_f, state):
    b, h, n_tok, d = q.shape
    n_chunks = n_tok // MLSTM_CHUNK

    def to_chunks(a):
        return jnp.moveaxis(a.reshape(b, h, n_chunks, MLSTM_CHUNK, *a.shape[3:]), 2, 0)

    xs = (to_chunks(q), to_chunks(k), to_chunks(v), to_chunks(log_i), to_chunks(log_f))
    lower = jnp.tril(jnp.ones((MLSTM_CHUNK, MLSTM_CHUNK), dtype=bool))

    def step(carry, inp):
        c_mat, n_vec, m = carry
        qc, kc, vc, li, lf = inp
        cum = jnp.cumsum(lf, axis=-1)
        d_mat = jnp.where(lower, cum[..., :, None] - cum[..., None, :] + li[..., None, :], -jnp.inf)
        inter = cum + m[..., None]
        m_t = jnp.maximum(inter, jnp.max(d_mat, axis=-1))
        w_intra = jnp.exp(d_mat - m_t[..., None])
        w_inter = jnp.exp(inter - m_t)
        s = jnp.einsum("bhtd,bhsd->bhts", qc, kc) * w_intra
        num = w_inter[..., None] * jnp.einsum("bhtd,bhde->bhte", qc, c_mat) + jnp.einsum("bhts,bhse->bhte", s, vc)
        den = w_inter * jnp.einsum("bhtd,bhd->bht", qc, n_vec) + jnp.sum(s, axis=-1)
        h_out = num / jnp.maximum(jnp.abs(den), jnp.exp(-m_t))[..., None]
        cum_last = cum[..., -1]
        g = cum_last[..., None] - cum + li
        m_new = jnp.maximum(cum_last + m, jnp.max(g, axis=-1))
        decay = jnp.exp(cum_last + m - m_new)
        wk = jnp.exp(g - m_new[..., None])[..., None] * kc
        c_new = decay[..., None, None] * c_mat + jnp.einsum("bhsd,bhse->bhde", wk, vc)
        n_new = decay[..., None] * n_vec + jnp.sum(wk, axis=-2)
        return (c_new, n_new, m_new), h_out

    state, hs = lax.scan(step, state, xs)
    return jnp.moveaxis(hs, 0, 2).reshape(b, h, n_tok, d), state


def mlstm_prepare(qkv, gate_pre, gate_b):
    b, n, _ = qkv.shape
    q, k, v = jnp.split(qkv.astype(jnp.float32), 3, axis=-1)

    def to_heads(a):
        return a.reshape(b, n, MLSTM_HEADS, MLSTM_HEAD_DIM).transpose(0, 2, 1, 3)

    pre = (gate_pre.astype(jnp.float32) + gate_b.astype(jnp.float32)).reshape(b, n, 4, MLSTM_HEADS).transpose(2, 0, 3, 1)
    return to_heads(q), to_heads(k) * MLSTM_HEAD_DIM ** -0.5, to_heads(v), pre


def mlstm_bidirectional(qx, kx, vx, pre_x, qc, kc, vc, pre_c):
    b = qx.shape[0]
    zero = (jnp.zeros((b, MLSTM_HEADS, MLSTM_HEAD_DIM, MLSTM_HEAD_DIM), jnp.float32),
            jnp.zeros((b, MLSTM_HEADS, MLSTM_HEAD_DIM), jnp.float32),
            jnp.full((b, MLSTM_HEADS), M_INIT, jnp.float32))

    def flip(a):
        return jnp.flip(a, axis=2)

    lsig = jax.nn.log_sigmoid
    hc_f, st_f = mlstm_chunked(qc, kc, vc, pre_c[0], lsig(pre_c[1]), zero)
    hx_f, _ = mlstm_chunked(qx, kx, vx, pre_x[0], lsig(pre_x[1]), st_f)
    hc_b, st_b = mlstm_chunked(flip(qc), flip(kc), flip(vc), flip(pre_c[2]), flip(lsig(pre_c[3])), zero)
    hx_b, _ = mlstm_chunked(flip(qx), flip(kx), flip(vx), flip(pre_x[2]), flip(lsig(pre_x[3])), st_b)
    return hx_f + flip(hx_b), hc_f + flip(hc_b)


def mlstm_readout(h, o_pre, norm_w, w_o):
    b, _, n, _ = h.shape
    hn = rmsnorm(h.transpose(0, 2, 1, 3), norm_w.reshape(MLSTM_HEADS, MLSTM_HEAD_DIM)).reshape(b, n, MLSTM_DIM)
    return (hn.astype(o_pre.dtype) * jax.nn.sigmoid(o_pre)) @ w_o


def attn_heads(a, n_heads, norm_w):
    b, n, _ = a.shape
    return rmsnorm(a.reshape(b, n, n_heads, ATTN_HEAD_DIM), norm_w)


def attend(q, k, v):
    s = jnp.einsum("bqhgd,bkhd->bhgqk", q, k).astype(jnp.float32) * ATTN_HEAD_DIM ** -0.5
    p = jax.nn.softmax(s, axis=-1).astype(v.dtype)
    return jnp.einsum("bhgqk,bkhd->bqhgd", p, v)


def latent_attention(q, k, v):
    b, n = q.shape[:2]
    n_blocks = n // Q_BLOCK
    qb = jnp.moveaxis(q.reshape(b, n_blocks, Q_BLOCK, ATTN_KV_HEADS, ATTN_GROUP, ATTN_HEAD_DIM), 1, 0)
    o = lax.map(lambda qi: attend(qi, k, v), qb)
    return jnp.moveaxis(o, 0, 1).reshape(b, n, ATTN_DIM)


def conformer_conv(u, dw_w, dw_b, ln_w, ln_b, w_o):
    a, g = jnp.split(u, 2, axis=-1)
    h = dwconv(a * jax.nn.sigmoid(g), dw_w, dw_b)
    return jax.nn.silu(layernorm(h, ln_w, ln_b)) @ w_o


def conv_ffn(h, w_up, dw_w, dw_b, w_down):
    u = dwconv(h @ w_up, dw_w, dw_b)
    a, g = jnp.split(u, 2, axis=-1)
    return (a * jax.nn.silu(g)) @ w_down


def token_mixers(hx, hc, cos, sin, want_ctx, w_in, branch_gate_b, conv_dw_w, conv_dw_b, conv_ln_w, conv_ln_b,
                 w_conv_out, mlstm_gate_b, mlstm_norm_w, w_mlstm_out, q_norm_w, k_norm_w, w_attn_out, w_out):
    offsets = [int(o) for o in np.cumsum(IN_SIZES)[:-1]]
    cu_x, mqkv_x, mo_x, mg_x, aq_x, ak_x, av_x, bg_x = jnp.split(hx @ w_in, offsets, axis=-1)
    cu_c, mqkv_c, mo_c, mg_c, aq_c, ak_c, av_c, bg_c = jnp.split(hc @ w_in, offsets, axis=-1)
    b, n, _ = hx.shape
    n_ctx = hc.shape[1]

    conv_x = conformer_conv(cu_x, conv_dw_w, conv_dw_b, conv_ln_w, conv_ln_b, w_conv_out)

    h_x, h_c = mlstm_bidirectional(*mlstm_prepare(mqkv_x, mg_x, mlstm_gate_b), *mlstm_prepare(mqkv_c, mg_c, mlstm_gate_b))
    mlstm_x = mlstm_readout(h_x, mo_x, mlstm_norm_w, w_mlstm_out)

    k_c = attn_heads(ak_c, ATTN_KV_HEADS, k_norm_w)
    v_c = av_c.reshape(b, n_ctx, ATTN_KV_HEADS, ATTN_HEAD_DIM)
    k_x = apply_rope(attn_heads(ak_x, ATTN_KV_HEADS, k_norm_w), cos, sin)
    v_x = av_x.reshape(b, n, ATTN_KV_HEADS, ATTN_HEAD_DIM)
    q_x = apply_rope(attn_heads(aq_x, ATTN_HEADS, q_norm_w), cos, sin).reshape(b, n, ATTN_KV_HEADS, ATTN_GROUP, ATTN_HEAD_DIM)
    attn_x = latent_attention(q_x, jnp.concatenate([k_c, k_x], axis=1), jnp.concatenate([v_c, v_x], axis=1)) @ w_attn_out

    def merge(conv_o, mlstm_o, attn_o, gate_pre):
        g_conv, g_mlstm, g_attn = jnp.split(jax.nn.sigmoid(gate_pre + branch_gate_b), N_BRANCH, axis=-1)
        return (g_conv * conv_o + g_mlstm * mlstm_o + g_attn * attn_o) @ w_out

    out_x = merge(conv_x, mlstm_x, attn_x, bg_x)
    if not want_ctx:
        return out_x, None
    conv_c = conformer_conv(cu_c, conv_dw_w, conv_dw_b, conv_ln_w, conv_ln_b, w_conv_out)
    mlstm_c = mlstm_readout(h_c, mo_c, mlstm_norm_w, w_mlstm_out)
    q_c = attn_heads(aq_c, ATTN_HEADS, q_norm_w).reshape(b, n_ctx, ATTN_KV_HEADS, ATTN_GROUP, ATTN_HEAD_DIM)
    attn_c = attend(q_c, k_c, v_c).reshape(b, n_ctx, ATTN_DIM) @ w_attn_out
    return out_x, merge(conv_c, mlstm_c, attn_c, bg_c)


def setup_inputs(seed: int = 0) -> dict:
    key = jax.random.key(seed)
    ks = jax.random.split(key, 32)
    f32 = jnp.float32
    D = D_MODEL
    L = DEPTH

    def nrm(i, shape, scale):
        return jax.random.normal(ks[i], shape, f32) * scale

    fb = jnp.linspace(3.0, 6.0, MLSTM_HEADS, dtype=f32)
    zh = jnp.zeros((MLSTM_HEADS,), f32)
    gate_base = jnp.concatenate([zh, fb, zh, fb])
    return {
        "x": nrm(0, (BATCH, SEQ, D), 1.0),
        "c": nrm(1, (BATCH, D), 1.0),
        "ctx": nrm(2, (BATCH, CTX_LEN, D), 1.0),
        "c_ctx": nrm(3, (D,), 1.0),
        "ada_w": nrm(4, (L, D, 6 * D), 0.5 * D ** -0.5),
        "ada_b": nrm(5, (L, 6 * D), 0.02),
        "norm1_w": 1.0 + nrm(6, (L, D), 0.02),
        "norm2_w": 1.0 + nrm(7, (L, D), 0.02),
        "w_in": nrm(8, (L, D, N_IN), D ** -0.5),
        "branch_gate_b": nrm(9, (L, N_BRANCH * D), 0.02),
        "conv_dw_w": nrm(10, (L, CONV_K, CONV_DIM), CONV_K ** -0.5),
        "conv_dw_b": nrm(11, (L, CONV_DIM), 0.02),
        "conv_ln_w": 1.0 + nrm(12, (L, CONV_DIM), 0.02),
        "conv_ln_b": nrm(13, (L, CONV_DIM), 0.02),
        "w_conv_out": nrm(14, (L, CONV_DIM, D), CONV_DIM ** -0.5),
        "mlstm_gate_b": gate_base + nrm(15, (L, 4 * MLSTM_HEADS), 0.1),
        "mlstm_norm_w": 1.0 + nrm(16, (L, MLSTM_DIM), 0.02),
        "w_mlstm_out": nrm(17, (L, MLSTM_DIM, D), MLSTM_DIM ** -0.5),
        "q_norm_w": 1.0 + nrm(18, (L, ATTN_HEAD_DIM), 0.02),
        "k_norm_w": 1.0 + nrm(19, (L, ATTN_HEAD_DIM), 0.02),
        "w_attn_out": nrm(20, (L, ATTN_DIM, D), ATTN_DIM ** -0.5),
        "w_out": nrm(21, (L, D, D), D ** -0.5),
        "ffn_w_up": nrm(22, (L, D, 2 * D_FF), D ** -0.5),
        "ffn_conv_w": nrm(23, (L, FFN_CONV_K, 2 * D_FF), FFN_CONV_K ** -0.5),
        "ffn_conv_b": nrm(24, (L, 2 * D_FF), 0.02),
        "ffn_w_down": nrm(25, (L, D_FF, D), D_FF ** -0.5),
    }


def reference(x, c, ctx, c_ctx, ada_w, ada_b, norm1_w, norm2_w, w_in, branch_gate_b, conv_dw_w, conv_dw_b,
              conv_ln_w, conv_ln_b, w_conv_out, mlstm_gate_b, mlstm_norm_w, w_mlstm_out, q_norm_w, k_norm_w,
              w_attn_out, w_out, ffn_w_up, ffn_conv_w, ffn_conv_b, ffn_w_down):
    rows = x.shape[1] // GRID_W
    cos, sin = axial_rope_tables(rows)
    xc = ctx
    for l in range(DEPTH):
        want_ctx = l < DEPTH - 1
        mod_x = jax.nn.silu(c) @ ada_w[l] + ada_b[l]
        mod_c = jax.nn.silu(c_ctx) @ ada_w[l] + ada_b[l]
        sh1, sc1, g1, sh2, sc2, g2 = jnp.split(mod_x[:, None, :], 6, axis=-1)
        sh1c, sc1c, g1c, sh2c, sc2c, g2c = jnp.split(mod_c, 6, axis=-1)

        hx = rmsnorm(x, norm1_w[l]) * (1.0 + sc1) + sh1
        hc = rmsnorm(xc, norm1_w[l]) * (1.0 + sc1c) + sh1c
        mix_x, mix_c = token_mixers(hx, hc, cos, sin, want_ctx, w_in[l], branch_gate_b[l], conv_dw_w[l],
                                    conv_dw_b[l], conv_ln_w[l], conv_ln_b[l], w_conv_out[l], mlstm_gate_b[l],
                                    mlstm_norm_w[l], w_mlstm_out[l], q_norm_w[l], k_norm_w[l], w_attn_out[l], w_out[l])
        x = x + g1 * mix_x
        hx = rmsnorm(x, norm2_w[l]) * (1.0 + sc2) + sh2
        x = x + g2 * conv_ffn(hx, ffn_w_up[l], ffn_conv_w[l], ffn_conv_b[l], ffn_w_down[l])
        if want_ctx:
            xc = xc + g1c * mix_c
            hc = rmsnorm(xc, norm2_w[l]) * (1.0 + sc2c) + sh2c
            xc = xc + g2c * conv_ffn(hc, ffn_w_up[l], ffn_conv_w[l], ffn_conv_b[l], ffn_w_down[l])
    return x
```

```python
import functools

import numpy as np
import jax
import jax.numpy as jnp
from jax import lax
from jax.experimental import pallas as pl
from jax.experimental.pallas import tpu as pltpu

F32 = jnp.float32
BF16 = jnp.bfloat16

GRID_W = 64
EPS = 1e-6
ROPE_THETA = 10000.0
M_INIT = -1e30
CONV_DIM = 512
CONV_K = 31
MLSTM_HEADS = 4
MLSTM_DIM = 1024
MLSTM_HEAD_DIM = MLSTM_DIM // MLSTM_HEADS
ATTN_HEADS = 8
ATTN_KV_HEADS = 2
ATTN_GROUP = ATTN_HEADS // ATTN_KV_HEADS
ATTN_HEAD_DIM = 64
ATTN_DIM = ATTN_HEADS * ATTN_HEAD_DIM
KV_DIM = ATTN_KV_HEADS * ATTN_HEAD_DIM
D_FF = 2816
FFN_CONV_K = 3
N_BRANCH = 3

LANES = 128
TM = 256
CONV_HALO = 16
FFN_HALO = 8
CONV_ROWS = 32
ATTN_TQ = 128
ATTN_KB = 256
FF_CHUNK = 256
ADA_TN = 1536
VMEM_LIMIT = 56 * 1024 * 1024


def _params(sem):
    return pltpu.CompilerParams(dimension_semantics=sem, vmem_limit_bytes=VMEM_LIMIT)


def _resident(shape):
    zeros = (0,) * len(shape)
    return pl.BlockSpec(shape, lambda *_: zeros, pipeline_mode=pl.Buffered(1))


def _dot(a, b):
    return jnp.dot(a, b, preferred_element_type=F32)


def _dot_nt(a, b):
    return lax.dot_general(a, b, (((1,), (1,)), ((), ())), preferred_element_type=F32)


def _dot_tn(a, b):
    return lax.dot_general(a, b, (((0,), (0,)), ((), ())), preferred_element_type=F32)


def _sigmoid(x):
    return 1.0 / (1.0 + jnp.exp(-x))


def _log_sigmoid(x):
    return jnp.minimum(x, 0.0) - jnp.log1p(jnp.exp(-jnp.abs(x)))


def _modulated_rmsnorm(x, w, shift, scale):
    y = x * lax.rsqrt(jnp.mean(x * x, axis=-1, keepdims=True) + EPS)
    return (y * w) * (1.0 + scale) + shift


def _ada_kernel(c_ref, w_ref, b_ref, o_ref):
    cv = c_ref[...]
    act = (cv * _sigmoid(cv)).astype(BF16)
    o_ref[0] = _dot(act, w_ref[0].astype(BF16)) + b_ref[0]


def _ada(cvecs, ada_w, ada_b):
    depth, d, n = ada_w.shape
    rows = cvecs.shape[0]
    return pl.pallas_call(
        _ada_kernel,
        grid=(depth, n // ADA_TN),
        in_specs=[pl.BlockSpec((rows, d), lambda l, j: (0, 0)),
                  pl.BlockSpec((1, d, ADA_TN), lambda l, j: (l, 0, j)),
                  pl.BlockSpec((1, 1, ADA_TN), lambda l, j: (l, 0, j))],
        out_specs=pl.BlockSpec((1, rows, ADA_TN), lambda l, j: (l, 0, j)),
        out_shape=jax.ShapeDtypeStruct((depth, rows, n), F32),
        compiler_params=_params(("parallel", "parallel")),
        name="ada_mod",
    )(cvecs, ada_w, ada_b.reshape(depth, 1, n))


def _swap_rotary_halves(x):
    half = ATTN_HEAD_DIM // 2
    width = x.shape[-1]
    lane = lax.broadcasted_iota(jnp.int32, x.shape, 1)
    first = (lane % ATTN_HEAD_DIM) < half
    return jnp.where(first, pltpu.roll(x, width - half, 1), pltpu.roll(x, half, 1))


def _head_norm_rope(a, bd_ref, w_ref, cos, sin):
    ss = _dot((a * a).astype(BF16), bd_ref[...])
    an = a * lax.rsqrt(ss * (1.0 / ATTN_HEAD_DIM) + EPS) * w_ref[...]
    reps = a.shape[-1] // LANES
    cos_w = jnp.concatenate([cos] * reps, axis=-1)
    sin_w = jnp.concatenate([sin] * reps, axis=-1)
    return an * cos_w + _swap_rotary_halves(an) * sin_w


def _inproj_kernel(x_ref, mod_ref, n1w_ref, wcu_ref, wmqkv_ref, wmo_ref, wmgt_ref, mgb_ref, waq_ref,
                   wak_ref, wav_ref, wbg_ref, bgb_ref, qnw_ref, knw_ref, bdq_ref, bdk_ref, cos_ref,
                   sin_ref, vone_ref, glu_ref, mqkv_ref, sigo_ref, gates_ref, q_ref, k_ref, v_ref,
                   bg_ref):
    h = _modulated_rmsnorm(x_ref[0], n1w_ref[...], mod_ref[0, 0:1, :], mod_ref[0, 1:2, :])
    hb = h.astype(BF16)

    a = _dot(hb, wcu_ref[:, :CONV_DIM])
    g = _dot(hb, wcu_ref[:, CONV_DIM:])
    glu_ref[0] = (a * _sigmoid(g)).astype(glu_ref.dtype)

    for i in range(3):
        r = _dot(hb, wmqkv_ref[:, i * MLSTM_DIM:(i + 1) * MLSTM_DIM])
        if i == 1:
            r = r * (MLSTM_HEAD_DIM ** -0.5)
        mqkv_ref[0, :, i * MLSTM_DIM:(i + 1) * MLSTM_DIM] = r.astype(mqkv_ref.dtype)
    sigo_ref[0] = _sigmoid(_dot(hb, wmo_ref[...])).astype(sigo_ref.dtype)

    gt = _dot_nt(wmgt_ref[...], hb) + mgb_ref[...]
    row = lax.broadcasted_iota(jnp.int32, gt.shape, 0)
    is_forget = (row // MLSTM_HEADS) % 2 == 1
    gates_ref[0] = jnp.where(is_forget, _log_sigmoid(gt), gt)

    cos = cos_ref[...]
    sin = sin_ref[...]
    q = _head_norm_rope(_dot(hb, waq_ref[...]), bdq_ref, qnw_ref, cos, sin)
    q_ref[0] = (q * (ATTN_HEAD_DIM ** -0.5)).astype(q_ref.dtype)
    k = _head_norm_rope(_dot(hb, wak_ref[...]), bdk_ref, knw_ref, cos, sin)
    k_ref[0] = k.astype(k_ref.dtype)
    v_ref[0] = (_dot(hb, wav_ref[...]) + vone_ref[...]).astype(v_ref.dtype)

    for i in range(N_BRANCH):
        d = x_ref.shape[-1]
        r = _dot(hb, wbg_ref[:, i * d:(i + 1) * d]) + bgb_ref[:, i * d:(i + 1) * d]
        bg_ref[0, :, i * d:(i + 1) * d] = _sigmoid(r).astype(bg_ref.dtype)


def _block_diag_ones(width, block):
    idx = np.arange(width) // block
    return jnp.asarray((idx[:, None] == idx[None, :]).astype(np.float32), dtype=BF16)


def _inproj(xa, modsel, n1w, w, cos_t, sin_t, nt, nc):
    b, s, d = xa.shape
    kvw = ATTN_KV_HEADS * LANES
    tile = lambda width: pl.BlockSpec((1, TM, width), lambda bi, j: (bi, j, 0))
    in_specs = [
        tile(d),
        pl.BlockSpec((1, 6, d), lambda bi, j: (bi * 2 + (j >= nc).astype(jnp.int32), 0, 0)),
        _resident((1, d)),
        _resident(w["cu"].shape), _resident(w["mqkv"].shape), _resident(w["mo"].shape),
        _resident(w["mgt"].shape), _resident(w["mgb"].shape), _resident(w["aq"].shape),
        _resident(w["ak"].shape), _resident(w["av"].shape), _resident(w["bg"].shape),
        _resident(w["bgb"].shape), _resident(w["qnw"].shape), _resident(w["knw"].shape),
        _resident((ATTN_DIM, ATTN_DIM)), _resident((kvw, kvw)),
        pl.BlockSpec((TM, LANES), lambda bi, j: (j, 0)),
        pl.BlockSpec((TM, LANES), lambda bi, j: (j, 0)),
        _resident((1, kvw)),
    ]
    out_shape = [
        jax.ShapeDtypeStruct((b, s, CONV_DIM), BF16),
        jax.ShapeDtypeStruct((b, s, 3 * MLSTM_DIM), BF16),
        jax.ShapeDtypeStruct((b, s, MLSTM_DIM), BF16),
        jax.ShapeDtypeStruct((b, 4 * MLSTM_HEADS, s), F32),
        jax.ShapeDtypeStruct((b, s, ATTN_DIM), BF16),
        jax.ShapeDtypeStruct((b, s, kvw), BF16),
        jax.ShapeDtypeStruct((b, s, kvw), BF16),
        jax.ShapeDtypeStruct((b, s, N_BRANCH * d), BF16),
    ]
    out_specs = [tile(CONV_DIM), tile(3 * MLSTM_DIM), tile(MLSTM_DIM),
                 pl.BlockSpec((1, 4 * MLSTM_HEADS, TM), lambda bi, j: (bi, 0, j)),
                 tile(ATTN_DIM), tile(kvw), tile(kvw), tile(N_BRANCH * d)]
    vone = np.zeros((1, kvw), np.float32)
    vone[0, ATTN_HEAD_DIM::LANES] = 1.0
    return pl.pallas_call(
        _inproj_kernel,
        grid=(b, nt),
        in_specs=in_specs,
        out_specs=out_specs,
        out_shape=out_shape,
        compiler_params=_params(("parallel", "parallel")),
        name="in_proj",
    )(xa, modsel, n1w, w["cu"], w["mqkv"], w["mo"], w["mgt"], w["mgb"], w["aq"], w["ak"], w["av"],
      w["bg"], w["bgb"], w["qnw"], w["knw"], _block_diag_ones(ATTN_DIM, ATTN_HEAD_DIM),
      _block_diag_ones(kvw, ATTN_HEAD_DIM), cos_t, sin_t, jnp.asarray(vone))


def _conv_kernel(prev_ref, cur_ref, next_ref, w_ref, b_ref, lnw_ref, lnb_ref, o_ref, buf_ref, *, nc, nt, j0):
    j = pl.program_id(1) + j0
    has_prev = jnp.logical_and(j != 0, j != nc)
    has_next = jnp.logical_and(j != nc - 1, j != nt - 1)
    buf_ref[0:CONV_HALO] = jnp.where(has_prev, prev_ref[0].astype(F32), 0.0)
    buf_ref[CONV_HALO:CONV_HALO + TM] = cur_ref[0].astype(F32)
    buf_ref[CONV_HALO + TM:] = jnp.where(has_next, next_ref[0].astype(F32), 0.0)
    base = CONV_HALO - CONV_K // 2
    for r0 in range(0, TM, CONV_ROWS):
        acc = jnp.zeros((CONV_ROWS, CONV_DIM), F32)
        for k in range(CONV_K):
            acc = acc + w_ref[k:k + 1, :] * buf_ref[base + r0 + k:base + r0 + k + CONV_ROWS, :]
        acc = acc + b_ref[...]
        xc = acc - jnp.mean(acc, axis=-1, keepdims=True)
        y = xc * lax.rsqrt(jnp.mean(xc * xc, axis=-1, keepdims=True) + EPS)
        y = y * lnw_ref[...] + lnb_ref[...]
        o_ref[0, r0:r0 + CONV_ROWS, :] = (y * _sigmoid(y)).astype(o_ref.dtype)


def _conformer_conv(glu, dw_w, dw_b, ln_w, ln_b, nt, nc, j0):
    b, s, c = glu.shape
    hpt = TM // CONV_HALO
    last = s // CONV_HALO - 1
    kern = functools.partial(_conv_kernel, nc=nc, nt=nt, j0=j0)
    return pl.pallas_call(
        kern,
        grid=(b, nt - j0),
        in_specs=[
            pl.BlockSpec((1, CONV_HALO, c), lambda bi, j: (bi, jnp.maximum((j + j0) * hpt - 1, 0), 0)),
            pl.BlockSpec((1, TM, c), lambda bi, j: (bi, j + j0, 0)),
            pl.BlockSpec((1, CONV_HALO, c), lambda bi, j: (bi, jnp.minimum((j + j0 + 1) * hpt, last), 0)),
            _resident((CONV_K, c)), _resident((1, c)), _resident((1, c)), _resident((1, c)),
        ],
        out_specs=pl.BlockSpec((1, TM, c), lambda bi, j: (bi, j + j0, 0)),
        out_shape=jax.ShapeDtypeStruct((b, s, c), BF16),
        scratch_shapes=[pltpu.VMEM((TM + 2 * CONV_HALO, c), F32)],
        compiler_params=_params(("parallel", "parallel")),
        name="conformer_conv",
    )(glu, glu, glu, dw_w, dw_b.reshape(1, c), ln_w.reshape(1, c), ln_b.reshape(1, c))


def _mlstm_chunk(q, k, v, li_row, lf_row, c_ref, n_ref, m_ref, reverse):
    t = q.shape[0]
    ti = lax.broadcasted_iota(jnp.int32, (t, t), 0)
    si = lax.broadcasted_iota(jnp.int32, (t, t), 1)
    eye = ti == si
    sees = (si >= ti) if reverse else (si <= ti)
    seen = (ti >= si) if reverse else (ti <= si)
    lf_b = jnp.broadcast_to(lf_row, (t, t))
    li_b = jnp.broadcast_to(li_row, (t, t))
    lf_col = jnp.sum(jnp.where(eye, lf_b, 0.0), axis=1, keepdims=True)
    li_col = jnp.sum(jnp.where(eye, li_b, 0.0), axis=1, keepdims=True)
    cum_col = jnp.sum(jnp.where(sees, lf_b, 0.0), axis=1, keepdims=True)
    cum_row = jnp.sum(jnp.where(seen, lf_col, 0.0), axis=0, keepdims=True)
    total = jnp.sum(lf_row, axis=1, keepdims=True)
    m_prev = m_ref[0:1, 0:1]

    d_mat = jnp.where(sees, cum_col - cum_row + li_row, -jnp.inf)
    inter = cum_col + m_prev
    m_t = jnp.maximum(inter, jnp.max(d_mat, axis=1, keepdims=True))
    w_intra = jnp.exp(d_mat - m_t)
    w_inter = jnp.exp(inter - m_t)
    s = _dot_nt(q, k) * w_intra
    num = w_inter * _dot(q, c_ref[...].astype(BF16)) + _dot(s.astype(BF16), v)
    q_n = jnp.sum(q.astype(F32) * n_ref[...], axis=1, keepdims=True)
    den = w_inter * q_n + jnp.sum(s, axis=1, keepdims=True)
    h_out = num / jnp.maximum(jnp.abs(den), jnp.exp(-m_t))

    g_col = total - cum_col + li_col
    m_new = jnp.maximum(total + m_prev, jnp.max(g_col, axis=0, keepdims=True))
    decay = jnp.exp(total + m_prev - m_new)
    wk = jnp.exp(g_col - m_new) * k.astype(F32)
    c_ref[...] = decay * c_ref[...] + _dot_tn(wk.astype(BF16), v)
    n_ref[...] = decay * n_ref[...] + jnp.sum(wk, axis=0, keepdims=True)
    m_ref[...] = jnp.broadcast_to(m_new, m_ref.shape)
    return h_out


def _mlstm_kernel(qf_ref, kf_ref, vf_ref, gf_ref, qb_ref, kb_ref, vb_ref, gb_ref, hf_ref, hb_ref,
                  cf_ref, nf_ref, mf_ref, cb_ref, nb_ref, mb_ref):
    head = pl.program_id(1)

    @pl.when(pl.program_id(2) == 0)
    def _():
        for c_ref, n_ref, m_ref in ((cf_ref, nf_ref, mf_ref), (cb_ref, nb_ref, mb_ref)):
            c_ref[...] = jnp.zeros(c_ref.shape, F32)
            n_ref[...] = jnp.zeros(n_ref.shape, F32)
            m_ref[...] = jnp.full(m_ref.shape, M_INIT, F32)

    def gate_rows(g_ref, first):
        return g_ref[0, pl.ds(first + head, 1), :], g_ref[0, pl.ds(first + MLSTM_HEADS + head, 1), :]

    li, lf = gate_rows(gf_ref, 0)
    hf_ref[0] = _mlstm_chunk(qf_ref[0], kf_ref[0], vf_ref[0], li, lf, cf_ref, nf_ref, mf_ref, False)
    li, lf = gate_rows(gb_ref, 2 * MLSTM_HEADS)
    hb_ref[0] = _mlstm_chunk(qb_ref[0], kb_ref[0], vb_ref[0], li, lf, cb_ref, nb_ref, mb_ref, True)


def _mlstm(mqkv, gates, nt, nc):
    b, s, _ = mqkv.shape
    hd = MLSTM_HEAD_DIM

    def bwd_chunk(i):
        return jnp.where(i < nc, nc - 1 - i, nt - 1 - (i - nc))

    fwd = lambda col: pl.BlockSpec((1, TM, hd), lambda bi, h, i: (bi, i, col * MLSTM_HEADS + h))
    bwd = lambda col: pl.BlockSpec((1, TM, hd), lambda bi, h, i: (bi, bwd_chunk(i), col * MLSTM_HEADS + h))
    gate_f = pl.BlockSpec((1, 4 * MLSTM_HEADS, TM), lambda bi, h, i: (bi, 0, i))
    gate_b = pl.BlockSpec((1, 4 * MLSTM_HEADS, TM), lambda bi, h, i: (bi, 0, bwd_chunk(i)))
    state = [pltpu.VMEM((hd, hd), F32), pltpu.VMEM((1, hd), F32), pltpu.VMEM((8, LANES), F32)]
    return pl.pallas_call(
        _mlstm_kernel,
        grid=(b, MLSTM_HEADS, nt),
        in_specs=[fwd(0), fwd(1), fwd(2), gate_f, bwd(0), bwd(1), bwd(2), gate_b],
        out_specs=[pl.BlockSpec((1, TM, hd), lambda bi, h, i: (bi, i, h)),
                   pl.BlockSpec((1, TM, hd), lambda bi, h, i: (bi, bwd_chunk(i), h))],
        out_shape=[jax.ShapeDtypeStruct((b, s, MLSTM_DIM), F32)] * 2,
        scratch_shapes=state + state,
        compiler_params=_params(("parallel", "parallel", "arbitrary")),
        name="mlstm_scan",
    )(mqkv, mqkv, mqkv, gates, mqkv, mqkv, mqkv, gates)


def _attn_kernel(q_ref, k_ref, v_ref, o_ref, *, ctx_qtiles, ctx_kblocks, all_kblocks, j0):
    j = pl.program_id(2) + j0
    n_kblocks = jnp.where(j < ctx_qtiles, ctx_kblocks, all_kblocks)
    q = q_ref[0]
    low = lax.broadcasted_iota(jnp.int32, (ATTN_TQ, LANES), 1) < ATTN_HEAD_DIM
    zero = jnp.zeros((ATTN_TQ, LANES), q.dtype)
    parts = []
    for p in range(ATTN_GROUP * ATTN_HEAD_DIM // LANES):
        slab = q[:, p * LANES:(p + 1) * LANES]
        parts += [jnp.where(low, slab, zero), jnp.where(low, zero, slab)]
    qs = jnp.concatenate(parts, axis=0)

    def body(i, carry):
        m, acc = carry
        start = pl.multiple_of(i * ATTN_KB, ATTN_KB)
        s = _dot_nt(qs, k_ref[0, pl.ds(start, ATTN_KB), :])
        m_new = jnp.maximum(m, jnp.max(s, axis=1, keepdims=True))
        p = jnp.exp(s - m_new)
        acc = jnp.exp(m - m_new) * acc + _dot(p.astype(BF16), v_ref[0, pl.ds(start, ATTN_KB), :])
        return m_new, acc

    rows = ATTN_GROUP * ATTN_TQ
    m0 = jnp.full((rows, 1), -jnp.inf, F32)
    acc0 = jnp.zeros((rows, LANES), F32)
    _, acc = lax.fori_loop(0, n_kblocks, body, (m0, acc0))
    out = acc * (1.0 / acc[:, ATTN_HEAD_DIM:ATTN_HEAD_DIM + 1])
    for hh in range(ATTN_GROUP):
        o_ref[0, :, hh * LANES:(hh + 1) * LANES] = out[hh * ATTN_TQ:(hh + 1) * ATTN_TQ].astype(o_ref.dtype)


def _attention(q, k, v, n_ctx, j0_tokens):
    b, s, _ = q.shape
    j0 = j0_tokens // ATTN_TQ
    kern = functools.partial(_attn_kernel, ctx_qtiles=n_ctx // ATTN_TQ, ctx_kblocks=n_ctx // ATTN_KB,
                             all_kblocks=s // ATTN_KB, j0=j0)
    gw = ATTN_GROUP * ATTN_HEAD_DIM
    return pl.pallas_call(
        kern,
        grid=(b, ATTN_KV_HEADS, s // ATTN_TQ - j0),
        in_specs=[pl.BlockSpec((1, ATTN_TQ, gw), lambda bi, g, j: (bi, j + j0, g)),
                  pl.BlockSpec((1, s, LANES), lambda bi, g, j: (bi, 0, g)),
                  pl.BlockSpec((1, s, LANES), lambda bi, g, j: (bi, 0, g))],
        out_specs=pl.BlockSpec((1, ATTN_TQ, ATTN_GROUP * LANES), lambda bi, g, j: (bi, j + j0, g)),
        out_shape=jax.ShapeDtypeStruct((b, s, ATTN_HEADS * LANES), BF16),
        compiler_params=_params(("parallel", "parallel", "parallel")),
        name="gqa_attention",
    )(q, k, v)


def _merge_kernel(x_ref, mod_ref, conv_ref, hf_ref, hb_ref, sigo_ref, attn_ref, bg_ref, mnw_ref, wco_ref,
                  wmo_ref, wao_ref, wout_ref, o_ref):
    d = x_ref.shape[-1]
    conv_o = _dot(conv_ref[0], wco_ref[...])
    h = hf_ref[0] + hb_ref[0]
    parts = []
    for i in range(MLSTM_HEADS):
        hh = h[:, i * MLSTM_HEAD_DIM:(i + 1) * MLSTM_HEAD_DIM]
        parts.append(hh * lax.rsqrt(jnp.mean(hh * hh, axis=-1, keepdims=True) + EPS))
    hn = jnp.concatenate(parts, axis=-1) * mnw_ref[...]
    mlstm_o = _dot((hn * sigo_ref[0].astype(F32)).astype(BF16), wmo_ref[...])
    attn_o = _dot(attn_ref[0], wao_ref[...])
    merged = (bg_ref[0, :, 0:d].astype(F32) * conv_o + bg_ref[0, :, d:2 * d].astype(F32) * mlstm_o
              + bg_ref[0, :, 2 * d:3 * d].astype(F32) * attn_o)
    mix = _dot(merged.astype(BF16), wout_ref[...])
    o_ref[0] = x_ref[0] + mod_ref[0, 2:3, :] * mix


def _merge(xa, modsel, conv_a, hf, hb, sigo, attn, bgates, mnw, w, nt, nc, j0):
    b, s, d = xa.shape
    tile = lambda width: pl.BlockSpec((1, TM, width), lambda bi, j: (bi, j + j0, 0))
    return pl.pallas_call(
        _merge_kernel,
        grid=(b, nt - j0),
        in_specs=[tile(d),
                  pl.BlockSpec((1, 6, d), lambda bi, j: (bi * 2 + (j + j0 >= nc).astype(jnp.int32), 0, 0)),
                  tile(CONV_DIM), tile(MLSTM_DIM), tile(MLSTM_DIM), tile(MLSTM_DIM),
                  tile(ATTN_HEADS * LANES), tile(N_BRANCH * d), _resident((1, MLSTM_DIM)),
                  _resident(w["co"].shape), _resident(w["mout"].shape), _resident(w["ao"].shape),
                  _resident(w["out"].shape)],
        out_specs=tile(d),
        out_shape=jax.ShapeDtypeStruct((b, s, d), F32),
        compiler_params=_params(("parallel", "parallel")),
        name="merge",
    )(xa, modsel, conv_a, hf, hb, sigo, attn, bgates, mnw, w["co"], w["mout"], w["ao"], w["out"])


def _ffn_kernel(prev_ref, x_ref, next_ref, mod_ref, n2w_ref, wup_ref, cw_ref, cb_ref, wdn_ref, o_ref,
                *, nc, nt, j0):
    j = pl.program_id(1) + j0
    has_prev = jnp.logical_and(j != 0, j != nc)
    has_next = jnp.logical_and(j != nc - 1, j != nt - 1)
    shift, scale, gate = mod_ref[0, 3:4, :], mod_ref[0, 4:5, :], mod_ref[0, 5:6, :]
    norm = lambda v: _modulated_rmsnorm(v, n2w_ref[...], shift, scale)
    x = x_ref[0]
    hcat = jnp.concatenate([jnp.where(has_prev, norm(prev_ref[0]), 0.0), norm(x),
                            jnp.where(has_next, norm(next_ref[0]), 0.0)], axis=0).astype(BF16)
    rows = TM + 2 * FFN_HALO
    n_chunks = D_FF // FF_CHUNK

    def conv3(u, c):
        w = cw_ref[c]
        y = (w[0:1] * pltpu.roll(u, 1, 0) + w[1:2] * u + w[2:3] * pltpu.roll(u, rows - 1, 0))
        return y[FFN_HALO:FFN_HALO + TM] + cb_ref[c]

    def body(c, acc):
        ya = conv3(_dot(hcat, wup_ref[c]), c)
        yg = conv3(_dot(hcat, wup_ref[n_chunks + c]), n_chunks + c)
        act = ya * (yg * _sigmoid(yg))
        return acc + _dot(act.astype(BF16), wdn_ref[c])

    out = lax.fori_loop(0, n_chunks, body, jnp.zeros((TM, x.shape[-1]), F32))
    o_ref[0] = x + gate * out


def _conv_ffn(x1, modsel, n2w, w, nt, nc, j0, latent_only):
    b, s, d = x1.shape
    hpt = TM // FFN_HALO
    last = s // FFN_HALO - 1
    kern = functools.partial(_ffn_kernel, nc=nc, nt=nt, j0=j0)
    if latent_only:
        out_shape = jax.ShapeDtypeStruct((b, s - nc * TM, d), F32)
        out_spec = pl.BlockSpec((1, TM, d), lambda bi, j: (bi, j + j0 - nc, 0))
    else:
        out_shape = jax.ShapeDtypeStruct((b, s, d), F32)
        out_spec = pl.BlockSpec((1, TM, d), lambda bi, j: (bi, j + j0, 0))
    return pl.pallas_call(
        kern,
        grid=(b, nt - j0),
        in_specs=[
            pl.BlockSpec((1, FFN_HALO, d), lambda bi, j: (bi, jnp.maximum((j + j0) * hpt - 1, j0 * hpt), 0)),
            pl.BlockSpec((1, TM, d), lambda bi, j: (bi, j + j0, 0)),
            pl.BlockSpec((1, FFN_HALO, d), lambda bi, j: (bi, jnp.minimum((j + j0 + 1) * hpt, last), 0)),
            pl.BlockSpec((1, 6, d), lambda bi, j: (bi * 2 + (j + j0 >= nc).astype(jnp.int32), 0, 0)),
            _resident((1, d)), _resident(w["up"].shape), _resident(w["fcw"].shape),
            _resident(w["fcb"].shape), _resident(w["down"].shape)],
        out_specs=out_spec,
        out_shape=out_shape,
        compiler_params=_params(("parallel", "parallel")),
        name="conv_ffn",
    )(x1, x1, x1, modsel, n2w, w["up"], w["fcw"], w["fcb"], w["down"])


def _rope_tables(n_ctx, seq):
    rows = seq // GRID_W
    row = jnp.broadcast_to(jnp.arange(rows, dtype=F32)[:, None], (rows, GRID_W)).reshape(-1)
    col = jnp.broadcast_to(jnp.arange(GRID_W, dtype=F32)[None, :], (rows, GRID_W)).reshape(-1)
    n_freq = ATTN_HEAD_DIM // 4
    inv_freq = ROPE_THETA ** (-jnp.arange(n_freq, dtype=F32) / n_freq)
    ang = jnp.concatenate([row[:, None] * inv_freq, col[:, None] * inv_freq], axis=-1)
    cos, sin = jnp.cos(ang), jnp.sin(ang)
    reps = LANES // ATTN_HEAD_DIM
    cos_t = jnp.tile(jnp.concatenate([cos, cos], axis=-1), (1, reps))
    sin_t = jnp.tile(jnp.concatenate([-sin, sin], axis=-1), (1, reps))
    cos_t = jnp.concatenate([jnp.ones((n_ctx, LANES), F32), cos_t], axis=0)
    sin_t = jnp.concatenate([jnp.zeros((n_ctx, LANES), F32), sin_t], axis=0)
    return cos_t, sin_t


def _layer_weights(l, d, w_in, branch_gate_b, mlstm_gate_b, q_norm_w, k_norm_w, w_conv_out, w_mlstm_out,
                   w_attn_out, w_out, ffn_w_up, ffn_conv_w, ffn_conv_b, ffn_w_down):
    sizes = (2 * CONV_DIM, 3 * MLSTM_DIM, MLSTM_DIM, 4 * MLSTM_HEADS, ATTN_DIM, KV_DIM, KV_DIM, N_BRANCH * d)
    offs = [int(o) for o in np.cumsum(sizes)[:-1]]
    cu, mqkv, mo, mg, aq, ak, av, bg = jnp.split(w_in[l].astype(BF16), offs, axis=-1)
    hd = ATTN_HEAD_DIM
    pad = jnp.zeros((d, LANES - hd), BF16)
    ak_dup = jnp.concatenate([ak[:, i * hd:(i + 1) * hd] for i in range(ATTN_KV_HEADS) for _ in range(LANES // hd)], axis=-1)
    av_ext = jnp.concatenate([p for i in range(ATTN_KV_HEADS) for p in (av[:, i * hd:(i + 1) * hd], pad)], axis=-1)
    wao = w_attn_out[l].astype(BF16).reshape(ATTN_HEADS, hd, d)
    wao = jnp.concatenate([wao, jnp.zeros((ATTN_HEADS, LANES - hd, d), BF16)], axis=1).reshape(ATTN_HEADS * LANES, d)
    n_chunks = D_FF // FF_CHUNK
    chunked = lambda a: a.reshape(a.shape[0], 2 * n_chunks, FF_CHUNK).transpose(1, 0, 2)
    return {
        "cu": cu, "mqkv": mqkv, "mo": mo, "mgt": mg.T, "mgb": mlstm_gate_b[l].reshape(-1, 1),
        "aq": aq, "ak": ak_dup, "av": av_ext, "bg": bg, "bgb": branch_gate_b[l].reshape(1, -1),
        "qnw": jnp.tile(q_norm_w[l], ATTN_HEADS).reshape(1, -1),
        "knw": jnp.tile(k_norm_w[l], ATTN_KV_HEADS * LANES // hd).reshape(1, -1),
        "co": w_conv_out[l].astype(BF16), "mout": w_mlstm_out[l].astype(BF16), "ao": wao,
        "out": w_out[l].astype(BF16),
        "up": chunked(ffn_w_up[l].astype(BF16)),
        "fcw": chunked(ffn_conv_w[l]),
        "fcb": ffn_conv_b[l].reshape(2 * n_chunks, 1, FF_CHUNK),
        "down": ffn_w_down[l].astype(BF16).reshape(n_chunks, FF_CHUNK, d),
    }


def kernel(x, c, ctx, c_ctx, ada_w, ada_b, norm1_w, norm2_w, w_in, branch_gate_b, conv_dw_w, conv_dw_b, conv_ln_w, conv_ln_b, w_conv_out, mlstm_gate_b, mlstm_norm_w, w_mlstm_out, q_norm_w, k_norm_w, w_attn_out, w_out, ffn_w_up, ffn_conv_w, ffn_conv_b, ffn_w_down):
    b, seq, d = x.shape
    n_ctx = ctx.shape[1]
    depth = ada_w.shape[0]
    assert seq % TM == 0 and n_ctx % TM == 0 and seq % GRID_W == 0
    nt, nc = (n_ctx + seq) // TM, n_ctx // TM

    pad_rows = -(b + 1) % 8
    cvecs = jnp.concatenate([c, c_ctx[None, :], jnp.zeros((pad_rows, d), F32)], axis=0)
    mods = _ada(cvecs, ada_w, ada_b).reshape(depth, -1, 6, d)
    modsel = jnp.stack([jnp.broadcast_to(mods[:, b:b + 1], (depth, b, 6, d)), mods[:, :b]], axis=2)
    modsel = modsel.reshape(depth, 2 * b, 6, d)

    cos_t, sin_t = _rope_tables(n_ctx, seq)
    xa = jnp.concatenate([ctx, x], axis=1)
    for l in range(depth):
        last = l == depth - 1
        j0 = nc if last else 0
        w = _layer_weights(l, d, w_in, branch_gate_b, mlstm_gate_b, q_norm_w, k_norm_w, w_conv_out,
                           w_mlstm_out, w_attn_out, w_out, ffn_w_up, ffn_conv_w, ffn_conv_b, ffn_w_down)
        glu, mqkv, sigo, gates, q, k, v, bgates = _inproj(xa, modsel[l], norm1_w[l].reshape(1, d), w,
                                                          cos_t, sin_t, nt, nc)
        conv_a = _conformer_conv(glu, conv_dw_w[l], conv_dw_b[l], conv_ln_w[l], conv_ln_b[l], nt, nc, j0)
        hf, hb = _mlstm(mqkv, gates, nt, nc)
        attn = _attention(q, k, v, n_ctx, j0 * TM)
        x1 = _merge(xa, modsel[l], conv_a, hf, hb, sigo, attn, bgates, mlstm_norm_w[l].reshape(1, -1), w,
                    nt, nc, j0)
        xa = _conv_ffn(x1, modsel[l], norm2_w[l].reshape(1, d), w, nt, nc, j0, latent_only=last)
    return xa
```

```python
import functools

import numpy as np
import jax
import jax.numpy as jnp
from jax import lax
from jax.experimental import pallas as pl
from jax.experimental.pallas import tpu as pltpu

F32 = jnp.float32
BF16 = jnp.bfloat16

GRID_W = 64
EPS = 1e-6
ROPE_THETA = 10000.0
M_INIT = -1e30
CONV_DIM = 512
CONV_K = 31
MLSTM_HEADS = 4
MLSTM_DIM = 1024
MLSTM_HEAD_DIM = MLSTM_DIM // MLSTM_HEADS
ATTN_HEADS = 8
ATTN_KV_HEADS = 2
ATTN_GROUP = ATTN_HEADS // ATTN_KV_HEADS
ATTN_HEAD_DIM = 64
ATTN_DIM = ATTN_HEADS * ATTN_HEAD_DIM
KV_DIM = ATTN_KV_HEADS * ATTN_HEAD_DIM
D_FF = 2816
FFN_CONV_K = 3
N_BRANCH = 3

LANES = 128
SUBLANES = 8
TM = 256
CONV_HALO = 16
FFN_HALO = 8
CONV_ROWS = 32
LOG2E = 1.4426950408889634
ATTN_TQ = 256
ATTN_KB_CHOICES = (1024, 512, 256)
FF_CHUNK = 256
ADA_TN = 1536
VMEM_LIMIT = 56 * 1024 * 1024


def _params(sem):
    return pltpu.CompilerParams(dimension_semantics=sem, vmem_limit_bytes=VMEM_LIMIT)


def _resident(shape):
    zeros = (0,) * len(shape)
    return pl.BlockSpec(shape, lambda *_: zeros, pipeline_mode=pl.Buffered(1))


def _dot(a, b):
    return jnp.dot(a, b, preferred_element_type=F32)


def _dot_nt(a, b):
    return lax.dot_general(a, b, (((1,), (1,)), ((), ())), preferred_element_type=F32)


def _dot_tn(a, b):
    return lax.dot_general(a, b, (((0,), (0,)), ((), ())), preferred_element_type=F32)


def _sigmoid(x):
    return 1.0 / (1.0 + jnp.exp(-x))


def _log_sigmoid(x):
    return jnp.minimum(x, 0.0) - jnp.log1p(jnp.exp(-jnp.abs(x)))


def _modulated_rmsnorm(x, w, shift, scale):
    y = x * lax.rsqrt(jnp.mean(x * x, axis=-1, keepdims=True) + EPS)
    return (y * w) * (1.0 + scale) + shift


def _stream_specs(xc, nc, j0):
    d = xc.shape[-1]
    lat0 = nc if xc.shape[1] > nc * TM else 0
    return [pl.BlockSpec((1, TM, d), lambda bi, j: (bi, jnp.minimum(j + j0, nc - 1), 0)),
            pl.BlockSpec((1, TM, d), lambda bi, j: (bi, jnp.maximum(j + j0 - nc, 0) + lat0, 0))]


def _stream_tile(xc_ref, xl_ref, nc, j0):
    if j0 >= nc:
        return xl_ref[0]
    return jnp.where(pl.program_id(1) + j0 < nc, xc_ref[0], xl_ref[0])


def _ada_kernel(c_ref, w_ref, b_ref, o_ref):
    cv = c_ref[...]
    act = (cv * _sigmoid(cv)).astype(BF16)
    o_ref[0] = _dot(act, w_ref[0].astype(BF16)) + b_ref[0]


def _ada(cvecs, ada_w, ada_b):
    depth, d, n = ada_w.shape
    rows = cvecs.shape[0]
    return pl.pallas_call(
        _ada_kernel,
        grid=(depth, n // ADA_TN),
        in_specs=[pl.BlockSpec((rows, d), lambda l, j: (0, 0)),
                  pl.BlockSpec((1, d, ADA_TN), lambda l, j: (l, 0, j)),
                  pl.BlockSpec((1, 1, ADA_TN), lambda l, j: (l, 0, j))],
        out_specs=pl.BlockSpec((1, rows, ADA_TN), lambda l, j: (l, 0, j)),
        out_shape=jax.ShapeDtypeStruct((depth, rows, n), F32),
        compiler_params=_params(("parallel", "parallel")),
        name="ada_mod",
    )(cvecs, ada_w, ada_b.reshape(depth, 1, n))


def _swap_rotary_halves(x):
    half = ATTN_HEAD_DIM // 2
    width = x.shape[-1]
    lane = lax.broadcasted_iota(jnp.int32, x.shape, 1)
    first = (lane % ATTN_HEAD_DIM) < half
    return jnp.where(first, pltpu.roll(x, width - half, 1), pltpu.roll(x, half, 1))


def _head_norm_rope(a, bd_ref, w_ref, cos, sin):
    ss = _dot((a * a).astype(BF16), bd_ref[...])
    an = a * lax.rsqrt(ss * (1.0 / ATTN_HEAD_DIM) + EPS) * w_ref[...]
    reps = a.shape[-1] // LANES
    cos_w = jnp.concatenate([cos] * reps, axis=-1)
    sin_w = jnp.concatenate([sin] * reps, axis=-1)
    return an * cos_w + _swap_rotary_halves(an) * sin_w


def _split3(x):
    hi = x.astype(BF16)
    rest = x - hi.astype(F32)
    mid = rest.astype(BF16)
    return hi, mid, (rest - mid.astype(F32)).astype(BF16)


def _inproj_kernel(xc_ref, xl_ref, mod_ref, n1w_ref, wcu_ref, wmqkv_ref, wmo_ref, wmgt_ref, mgbt_ref, waq_ref, wak_ref,
                   wav_ref, wbg_ref, bgb_ref, qnw_ref, knw_ref, bdq_ref, bdk_ref, cos_ref, sin_ref, vone_ref,
                   glu_ref, mqkv_ref, sigo_ref, grow_ref, gwf_ref, gwb_ref, q_ref, k_ref, v_ref, bg_ref, *, nc):
    x = _stream_tile(xc_ref, xl_ref, nc, 0)
    h = _modulated_rmsnorm(x, n1w_ref[...], mod_ref[0, 0:1, :], mod_ref[0, 1:2, :])
    hb = h.astype(BF16)

    a = _dot(hb, wcu_ref[:, :CONV_DIM])
    g = _dot(hb, wcu_ref[:, CONV_DIM:])
    glu_ref[0] = (a * _sigmoid(g)).astype(glu_ref.dtype)

    for i in range(3):
        r = _dot(hb, wmqkv_ref[:, i * MLSTM_DIM:(i + 1) * MLSTM_DIM])
        if i == 1:
            r = r * (MLSTM_HEAD_DIM ** -0.5)
        mqkv_ref[0, :, i * MLSTM_DIM:(i + 1) * MLSTM_DIM] = r.astype(mqkv_ref.dtype)
    sigo_ref[0] = _sigmoid(_dot(hb, wmo_ref[...])).astype(sigo_ref.dtype)

    ng, nh = wmgt_ref.shape[0], MLSTM_HEADS
    gt = _dot_nt(wmgt_ref[...], hb) + mgbt_ref[...]
    row = lax.broadcasted_iota(jnp.int32, gt.shape, 0)
    gt = jnp.where((row // nh) % 2 == 1, _log_sigmoid(gt), gt)
    ti = lax.broadcasted_iota(jnp.int32, (TM, TM), 0)
    si = lax.broadcasted_iota(jnp.int32, (TM, TM), 1)
    scans = _dot(jnp.concatenate(_split3(gt), axis=0), jnp.where(ti <= si, 1.0, 0.0).astype(BF16))
    prefix = scans[0:ng] + scans[ng:2 * ng] + scans[2 * ng:3 * ng]
    cum = jnp.where(row < ng // 2, prefix, prefix[:, TM - 1:TM] - prefix + gt)
    grow_ref[0, 0:ng, :] = gt
    grow_ref[0, ng:2 * ng, :] = cum
    cols = jnp.concatenate([gt, cum, jnp.zeros((LANES - 2 * ng, TM), F32)], axis=0).T
    for direction, gw_ref in enumerate((gwf_ref, gwb_ref)):
        for h in range(nh):
            gi = 2 * nh * direction + h
            for j, c in enumerate((gi, ng + gi + nh)):
                gw_ref[0, :, (2 * h + j) * LANES:(2 * h + j + 1) * LANES] = jnp.broadcast_to(cols[:, c:c + 1], (TM, LANES))

    cos = cos_ref[...]
    sin = sin_ref[...]
    q = _head_norm_rope(_dot(hb, waq_ref[...]), bdq_ref, qnw_ref, cos, sin)
    q_ref[0] = (q * (ATTN_HEAD_DIM ** -0.5 * LOG2E)).astype(q_ref.dtype)
    k = _head_norm_rope(_dot(hb, wak_ref[...]), bdk_ref, knw_ref, cos, sin)
    k_ref[0] = k.astype(k_ref.dtype)
    v_ref[0] = (_dot(hb, wav_ref[...]) + vone_ref[...]).astype(v_ref.dtype)

    for i in range(N_BRANCH):
        d = x.shape[-1]
        r = _dot(hb, wbg_ref[:, i * d:(i + 1) * d]) + bgb_ref[:, i * d:(i + 1) * d]
        bg_ref[0, :, i * d:(i + 1) * d] = _sigmoid(r).astype(bg_ref.dtype)


def _block_diag_ones(width, block):
    idx = np.arange(width) // block
    return jnp.asarray((idx[:, None] == idx[None, :]).astype(np.float32), dtype=BF16)


def _inproj(xc, xl, modsel, n1w, w, cos_t, sin_t, nt, nc):
    b, d = xc.shape[0], xc.shape[2]
    s = nt * TM
    kvw = ATTN_KV_HEADS * LANES
    ng = 4 * MLSTM_HEADS
    gww = 2 * MLSTM_HEADS * LANES
    tile = lambda width: pl.BlockSpec((1, TM, width), lambda bi, j: (bi, j, 0))
    in_specs = _stream_specs(xc, nc, 0) + [
        pl.BlockSpec((1, 6, d), lambda bi, j: (bi * 2 + (j >= nc).astype(jnp.int32), 0, 0)),
        _resident((1, d)),
        _resident(w["cu"].shape), _resident(w["mqkv"].shape), _resident(w["mo"].shape),
        _resident(w["mgt"].shape), _resident(w["mgbt"].shape), _resident(w["aq"].shape),
        _resident(w["ak"].shape), _resident(w["av"].shape), _resident(w["bg"].shape),
        _resident(w["bgb"].shape), _resident(w["qnw"].shape), _resident(w["knw"].shape),
        _resident((ATTN_DIM, ATTN_DIM)), _resident((kvw, kvw)),
        pl.BlockSpec((TM, LANES), lambda bi, j: (j, 0)),
        pl.BlockSpec((TM, LANES), lambda bi, j: (j, 0)),
        _resident((1, kvw)),
    ]
    out_shape = [
        jax.ShapeDtypeStruct((b, s, CONV_DIM), BF16),
        jax.ShapeDtypeStruct((b, s, 3 * MLSTM_DIM), BF16),
        jax.ShapeDtypeStruct((b, s, MLSTM_DIM), BF16),
        jax.ShapeDtypeStruct((b, 2 * ng, s), F32),
        jax.ShapeDtypeStruct((b, s, gww), F32),
        jax.ShapeDtypeStruct((b, s, gww), F32),
        jax.ShapeDtypeStruct((b, s, ATTN_DIM), BF16),
        jax.ShapeDtypeStruct((b, s, kvw), BF16),
        jax.ShapeDtypeStruct((b, s, kvw), BF16),
        jax.ShapeDtypeStruct((b, s, N_BRANCH * d), BF16),
    ]
    out_specs = [tile(CONV_DIM), tile(3 * MLSTM_DIM), tile(MLSTM_DIM),
                 pl.BlockSpec((1, 2 * ng, TM), lambda bi, j: (bi, 0, j)), tile(gww), tile(gww),
                 tile(ATTN_DIM), tile(kvw), tile(kvw), tile(N_BRANCH * d)]
    vone = np.zeros((1, kvw), np.float32)
    vone[0, ATTN_HEAD_DIM::LANES] = 1.0
    return pl.pallas_call(
        functools.partial(_inproj_kernel, nc=nc),
        grid=(b, nt),
        in_specs=in_specs,
        out_specs=out_specs,
        out_shape=out_shape,
        compiler_params=_params(("parallel", "parallel")),
        name="in_proj",
    )(xc, xl, modsel, n1w, w["cu"], w["mqkv"], w["mo"], w["mgt"], w["mgbt"], w["aq"], w["ak"],
      w["av"], w["bg"], w["bgb"], w["qnw"], w["knw"], _block_diag_ones(ATTN_DIM, ATTN_HEAD_DIM),
      _block_diag_ones(kvw, ATTN_HEAD_DIM), cos_t, sin_t, jnp.asarray(vone))


def _conv_tile(prev_ref, cur_ref, next_ref, w_ref, b_ref, lnw_ref, lnb_ref, buf_ref, sh_ref, act_ref, j, nc, nt):
    has_prev = jnp.logical_and(j != 0, j != nc)
    has_next = jnp.logical_and(j != nc - 1, j != nt - 1)
    buf_ref[0:CONV_HALO] = jnp.where(has_prev, prev_ref[0].astype(F32), 0.0)
    buf_ref[CONV_HALO:CONV_HALO + TM] = cur_ref[0].astype(F32)
    buf_ref[CONV_HALO + TM:] = jnp.where(has_next, next_ref[0].astype(F32), 0.0)
    n_sh = sh_ref.shape[1]
    for r in range(1, SUBLANES):
        sh_ref[r - 1] = buf_ref[r:r + n_sh, :]
    base = CONV_HALO - CONV_K // 2
    for r0 in range(0, TM, CONV_ROWS):
        acc = jnp.zeros((CONV_ROWS, CONV_DIM), F32)
        for k in range(CONV_K):
            r = (base + k) % SUBLANES
            row = r0 + base + k - r
            src = buf_ref[row:row + CONV_ROWS, :] if r == 0 else sh_ref[r - 1, row:row + CONV_ROWS, :]
            acc = acc + w_ref[k:k + 1, :] * src
        acc = acc + b_ref[...]
        xc = acc - jnp.mean(acc, axis=-1, keepdims=True)
        y = xc * lax.rsqrt(jnp.mean(xc * xc, axis=-1, keepdims=True) + EPS)
        y = y * lnw_ref[...] + lnb_ref[...]
        act_ref[r0:r0 + CONV_ROWS, :] = (y * _sigmoid(y)).astype(act_ref.dtype)


def _wide(col, n):
    return jnp.concatenate([col] * n, axis=1) if n > 1 else col


def _mlstm_chunks(q, k, v, li_row, cum_row, li_col, cum_col, c_ref, m_ref, reverse):
    each = lambda f, *args: [f(*a) for a in zip(*args)]
    t, d = q[0].shape
    n_slab, n_dslab = t // LANES, d // LANES
    slab = lambda x, j: x[:, j * LANES:(j + 1) * LANES]
    ti = lax.broadcasted_iota(jnp.int32, (t, t), 0)
    si = lax.broadcasted_iota(jnp.int32, (t, t), 1)
    sees = [(si >= ti) if r else (si <= ti) for r in reverse]
    ones = jnp.ones((t, LANES), BF16)
    total = [x[:, 0:1] if r else x[:, t - 1:t] for x, r in zip(cum_row, reverse)]
    m_prev = [r[0:1, :] for r in m_ref]

    row_term = each(lambda a, b: a - b, li_row, cum_row)
    d_mat = each(lambda m, c, r: [jnp.where(slab(m, j), c + slab(r, j), -jnp.inf) for j in range(n_slab)],
                 sees, cum_col, row_term)
    m_intra = [jnp.max(functools.reduce(jnp.maximum, dm), axis=1, keepdims=True) for dm in d_mat]
    inter = each(lambda c, m: c + m, cum_col, m_prev)
    m_t = each(lambda a, b: jnp.maximum(a, jnp.broadcast_to(b, (t, LANES))), inter, m_intra)
    w_inter = each(lambda a, b: jnp.exp(a - b), inter, m_t)
    qk = each(_dot_nt, q, k)
    s = each(lambda x, dm, m: jnp.concatenate([slab(x, j) * jnp.exp(dm[j] - m) for j in range(n_slab)],
                                              axis=1).astype(BF16), qk, d_mat, m_t)
    v_ext = [jnp.concatenate([x, ones], axis=1) for x in v]
    carried = each(lambda x, r: _dot(x, r[...].astype(BF16)), q, c_ref)
    intra = each(_dot, s, v_ext)
    den = each(lambda w, c, i: w * c[:, d:] + i[:, d:], w_inter, carried, intra)
    inv = each(lambda x, m: 1.0 / jnp.maximum(jnp.abs(x), jnp.exp(-m)), den, m_t)
    h_out = each(lambda w, c, i, r: (_wide(w, n_dslab) * c[:, :d] + i[:, :d]) * _wide(r, n_dslab),
                 w_inter, carried, intra, inv)

    g_col = each(lambda tot, c, i: tot - c + i, total, cum_col, li_col)
    m_new = each(lambda tot, m, g: jnp.maximum(tot + m, jnp.max(g, axis=0, keepdims=True)), total, m_prev, g_col)
    decay = each(lambda tot, m, mn: jnp.exp(tot + m - mn), total, m_prev, m_new)
    wk = each(lambda g, mn, x: (_wide(jnp.exp(g - mn), n_dslab) * x.astype(F32)).astype(BF16), g_col, m_new, k)
    update = each(_dot_tn, wk, v_ext)
    for i in range(len(q)):
        c_ref[i][...] = _wide(decay[i], n_dslab + 1) * c_ref[i][...] + update[i]
        m_ref[i][...] = jnp.broadcast_to(m_new[i], m_ref[i].shape)
    return h_out


def _mlstm_kernel(xf_ref, rf_ref, wf_ref, xb_ref, rb_ref, wb_ref, hf_ref, hb_ref, c_ref, m_ref):
    @pl.when(pl.program_id(1) == 0)
    def _():
        c_ref[...] = jnp.zeros(c_ref.shape, F32)
        m_ref[...] = jnp.full(m_ref.shape, M_INIT, F32)

    hd, nh = MLSTM_HEAD_DIM, MLSTM_HEADS
    ng = 4 * nh
    args = [[] for _ in range(7)]
    for h in range(nh):
        for reverse, (x_ref, r_ref, w_ref) in enumerate(((xf_ref, rf_ref, wf_ref), (xb_ref, rb_ref, wb_ref))):
            gi = 2 * nh * reverse + h
            vals = ([x_ref[0, :, (p * nh + h) * hd:(p * nh + h + 1) * hd] for p in range(3)]
                    + [r_ref[0, gi:gi + 1, :], r_ref[0, ng + gi + nh:ng + gi + nh + 1, :],
                       w_ref[0, :, 2 * h * LANES:(2 * h + 1) * LANES], w_ref[0, :, (2 * h + 1) * LANES:(2 * h + 2) * LANES]])
            for a, val in zip(args, vals):
                a.append(val)
    n_chain = 2 * nh
    h_out = _mlstm_chunks(*args, [c_ref.at[i] for i in range(n_chain)], [m_ref.at[i] for i in range(n_chain)],
                          [bool(i % 2) for i in range(n_chain)])
    for h in range(nh):
        hf_ref[0, :, h * hd:(h + 1) * hd] = h_out[2 * h].astype(hf_ref.dtype)
        hb_ref[0, :, h * hd:(h + 1) * hd] = h_out[2 * h + 1].astype(hb_ref.dtype)


def _mlstm(mqkv, grow, gwf, gwb, nt, nc):
    b, s, _ = mqkv.shape
    hd = MLSTM_HEAD_DIM
    n_chain = 2 * MLSTM_HEADS

    def bwd_chunk(i):
        return jnp.where(i < nc, nc - 1 - i, nt - 1 - (i - nc))

    fwd_i = lambda bi, i: (bi, i, 0)
    bwd_i = lambda bi, i: (bi, bwd_chunk(i), 0)
    specs = lambda im, gw: [pl.BlockSpec((1, TM, 3 * MLSTM_DIM), im),
                            pl.BlockSpec((1, grow.shape[1], TM), lambda bi, i: (bi, 0, im(bi, i)[1])),
                            pl.BlockSpec((1, TM, gw.shape[2]), im)]
    return pl.pallas_call(
        _mlstm_kernel,
        grid=(b, nt),
        in_specs=specs(fwd_i, gwf) + specs(bwd_i, gwb),
        out_specs=[pl.BlockSpec((1, TM, MLSTM_DIM), fwd_i), pl.BlockSpec((1, TM, MLSTM_DIM), bwd_i)],
        out_shape=[jax.ShapeDtypeStruct((b, s, MLSTM_DIM), BF16)] * 2,
        scratch_shapes=[pltpu.VMEM((n_chain, hd, hd + LANES), F32), pltpu.VMEM((n_chain, SUBLANES, LANES), F32)],
        compiler_params=_params(("parallel", "arbitrary")),
        name="mlstm_scan",
    )(mqkv, grow, gwf, mqkv, grow, gwb)


def _lane_fold_max(s):
    return functools.reduce(jnp.maximum, [s[:, i * LANES:(i + 1) * LANES] for i in range(s.shape[1] // LANES)])


def _exp2_bf16(s, m_wide):
    return jnp.concatenate([jnp.exp2(s[:, i * LANES:(i + 1) * LANES] - m_wide)
                            for i in range(s.shape[1] // LANES)], axis=1).astype(BF16)


def _stack_heads(q):
    low = lax.broadcasted_iota(jnp.int32, (q.shape[0], LANES), 1) < ATTN_HEAD_DIM
    zero = jnp.zeros((q.shape[0], LANES), q.dtype)
    parts = []
    for p in range(q.shape[1] // LANES):
        slab = q[:, p * LANES:(p + 1) * LANES]
        parts += [jnp.where(low, slab, zero), jnp.where(low, zero, slab)]
    return jnp.concatenate(parts, axis=0)


def _scores(q, k_ref, s_ref, m_ref, blocks):
    m = None
    for lo, hi in blocks:
        s = _dot_nt(q, k_ref[0, lo:hi, :])
        s_ref[:, lo:hi] = s
        fold = _lane_fold_max(s)
        m = fold if m is None else jnp.maximum(m, fold)
    m_ref[...] = m


def _weighted_values(s_ref, m_ref, v_ref, o_ref, blocks):
    m_wide = jnp.broadcast_to(jnp.max(m_ref[...], axis=1, keepdims=True), m_ref.shape)
    acc = None
    for lo, hi in blocks:
        part = _dot(_exp2_bf16(s_ref[:, lo:hi], m_wide), v_ref[0, lo:hi, :])
        acc = part if acc is None else acc + part
    tq = o_ref.shape[1]
    for hh in range(ATTN_GROUP):
        o_ref[0, :, hh * LANES:(hh + 1) * LANES] = acc[hh * tq:(hh + 1) * tq]


def _attn_kernel(q_ref, qnext_ref, k_ref, v_ref, o_ref, sa_ref, sb_ref, ma_ref, mb_ref, *, blocks):
    j = pl.program_id(2)

    @pl.when(j == 0)
    def _():
        _scores(_stack_heads(q_ref[0]), k_ref, sa_ref, ma_ref, blocks)

    for parity, (cur_s, cur_m, nxt_s, nxt_m) in enumerate(((sa_ref, ma_ref, sb_ref, mb_ref),
                                                           (sb_ref, mb_ref, sa_ref, ma_ref))):
        @pl.when(j % 2 == parity)
        def _():
            _scores(_stack_heads(qnext_ref[0]), k_ref, nxt_s, nxt_m, blocks)
            _weighted_values(cur_s, cur_m, v_ref, o_ref, blocks)


def _attn_ctx_kernel(q_ref, k_ref, v_ref, o_ref, s_ref, m_ref):
    blocks = [(0, k_ref.shape[1])]
    _scores(_stack_heads(q_ref[0]), k_ref, s_ref, m_ref, blocks)
    _weighted_values(s_ref, m_ref, v_ref, o_ref, blocks)


def _attention(q, k, v, n_ctx, want_ctx):
    b, s, _ = q.shape
    seq = s - n_ctx
    kb = next(c for c in ATTN_KB_CHOICES if seq % c == 0)
    blocks = [(0, n_ctx)] + [(lo, lo + kb) for lo in range(n_ctx, s, kb)]
    gw = ATTN_GROUP * ATTN_HEAD_DIM
    ow = ATTN_GROUP * LANES
    rows = ATTN_GROUP * ATTN_TQ
    j0, nq = n_ctx // ATTN_TQ, seq // ATTN_TQ
    keys = lambda n: pl.BlockSpec((1, n, LANES), lambda bi, g, j: (bi, 0, g))
    latent = pl.pallas_call(
        functools.partial(_attn_kernel, blocks=blocks),
        grid=(b, ATTN_KV_HEADS, nq),
        in_specs=[pl.BlockSpec((1, ATTN_TQ, gw), lambda bi, g, j: (bi, j + j0, g)),
                  pl.BlockSpec((1, ATTN_TQ, gw), lambda bi, g, j: (bi, jnp.minimum(j + 1, nq - 1) + j0, g)),
                  keys(s), keys(s)],
        out_specs=pl.BlockSpec((1, ATTN_TQ, ow), lambda bi, g, j: (bi, j, g)),
        out_shape=jax.ShapeDtypeStruct((b, seq, ATTN_HEADS * LANES), F32),
        scratch_shapes=[pltpu.VMEM((rows, s), F32), pltpu.VMEM((rows, s), F32),
                        pltpu.VMEM((rows, LANES), F32), pltpu.VMEM((rows, LANES), F32)],
        compiler_params=_params(("parallel", "parallel", "arbitrary")),
        name="gqa_attention",
    )(q, q, k, v)
    if not want_ctx:
        return latent, None
    context = pl.pallas_call(
        _attn_ctx_kernel,
        grid=(b, ATTN_KV_HEADS, j0),
        in_specs=[pl.BlockSpec((1, ATTN_TQ, gw), lambda bi, g, j: (bi, j, g)), keys(n_ctx), keys(n_ctx)],
        out_specs=pl.BlockSpec((1, ATTN_TQ, ow), lambda bi, g, j: (bi, j, g)),
        out_shape=jax.ShapeDtypeStruct((b, n_ctx, ATTN_HEADS * LANES), F32),
        scratch_shapes=[pltpu.VMEM((rows, n_ctx), F32), pltpu.VMEM((rows, LANES), F32)],
        compiler_params=_params(("parallel", "parallel", "parallel")),
        name="gqa_attention_ctx",
    )(q, k, v)
    return latent, context


def _merge_kernel(xc_ref, xl_ref, mod_ref, gprev_ref, gcur_ref, gnext_ref, dww_ref, dwb_ref, lnw_ref, lnb_ref,
                  hf_ref, hb_ref, sigo_ref, attn_c_ref, attn_l_ref, bg_ref, mnw_ref, wco_ref, wmo_ref, wao_ref,
                  wout_ref, o_ref, buf_ref, sh_ref, act_ref, *, nc, nt, j0):
    j = pl.program_id(1) + j0
    x = _stream_tile(xc_ref, xl_ref, nc, j0)
    d = x.shape[-1]
    attn = attn_l_ref[0]
    if j0 < nc:
        attn = jnp.where(j < nc, attn_c_ref[0], attn)
    h = hf_ref[0].astype(F32) + hb_ref[0].astype(F32)
    parts = []
    for i in range(MLSTM_HEADS):
        hh = h[:, i * MLSTM_HEAD_DIM:(i + 1) * MLSTM_HEAD_DIM]
        parts.append(hh * lax.rsqrt(jnp.mean(hh * hh, axis=-1, keepdims=True) + EPS))
    hn = jnp.concatenate(parts, axis=-1) * mnw_ref[...]
    mlstm_o = _dot((hn * sigo_ref[0].astype(F32)).astype(BF16), wmo_ref[...])
    attn = jnp.concatenate(
        [attn[:, i * LANES:(i + 1) * LANES] * (1.0 / attn[:, i * LANES + ATTN_HEAD_DIM:i * LANES + ATTN_HEAD_DIM + 1])
         for i in range(ATTN_HEADS)], axis=1).astype(BF16)
    attn_o = _dot(attn, wao_ref[...])
    _conv_tile(gprev_ref, gcur_ref, gnext_ref, dww_ref, dwb_ref, lnw_ref, lnb_ref, buf_ref, sh_ref, act_ref, j, nc, nt)
    conv_o = _dot(act_ref[...], wco_ref[...])
    merged = (bg_ref[0, :, 0:d].astype(F32) * conv_o + bg_ref[0, :, d:2 * d].astype(F32) * mlstm_o
              + bg_ref[0, :, 2 * d:3 * d].astype(F32) * attn_o)
    mix = _dot(merged.astype(BF16), wout_ref[...])
    o_ref[0] = x + mod_ref[0, 2:3, :] * mix


def _merge(xc, xl, modsel, glu, conv_p, hf, hb, sigo, attn_c, attn_l, bgates, mnw, w, nt, nc, j0):
    b, d = xc.shape[0], xc.shape[2]
    s = nt * TM
    tile = lambda width: pl.BlockSpec((1, TM, width), lambda bi, j: (bi, j + j0, 0))
    part = lambda width: pl.BlockSpec((1, TM, width), lambda bi, j: (bi, j, 0))
    aw = ATTN_HEADS * LANES
    c = glu.shape[2]
    hpt = TM // CONV_HALO
    last = s // CONV_HALO - 1
    return pl.pallas_call(
        functools.partial(_merge_kernel, nc=nc, nt=nt, j0=j0),
        grid=(b, nt - j0),
        in_specs=_stream_specs(xc, nc, j0) + [
                  pl.BlockSpec((1, 6, d), lambda bi, j: (bi * 2 + (j + j0 >= nc).astype(jnp.int32), 0, 0)),
                  pl.BlockSpec((1, CONV_HALO, c), lambda bi, j: (bi, jnp.maximum((j + j0) * hpt - 1, 0), 0)),
                  tile(c),
                  pl.BlockSpec((1, CONV_HALO, c), lambda bi, j: (bi, jnp.minimum((j + j0 + 1) * hpt, last), 0)),
                  _resident((CONV_K, c)), _resident((1, c)), _resident((1, c)), _resident((1, c)),
                  tile(MLSTM_DIM), tile(MLSTM_DIM), tile(MLSTM_DIM),
                  pl.BlockSpec((1, TM, aw), lambda bi, j: (bi, jnp.minimum(j + j0, nc - 1) if j0 < nc else 0, 0)),
                  pl.BlockSpec((1, TM, aw), lambda bi, j: (bi, jnp.maximum(j + j0 - nc, 0), 0)),
                  tile(N_BRANCH * d), _resident((1, MLSTM_DIM)),
                  _resident(w["co"].shape), _resident(w["mout"].shape), _resident(w["ao"].shape),
                  _resident(w["out"].shape)],
        out_specs=part(d),
        out_shape=jax.ShapeDtypeStruct((b, s - j0 * TM, d), F32),
        scratch_shapes=[pltpu.VMEM((TM + 2 * CONV_HALO, c), F32),
                        pltpu.VMEM((SUBLANES - 1, TM + 2 * CONV_HALO - SUBLANES, c), F32),
                        pltpu.VMEM((TM, c), BF16)],
        compiler_params=_params(("parallel", "parallel")),
        name="merge",
    )(xc, xl, modsel, glu, glu, glu, conv_p[0], conv_p[1].reshape(1, c), conv_p[2].reshape(1, c),
      conv_p[3].reshape(1, c), hf, hb, sigo, attn_c, attn_l, bgates, mnw, w["co"], w["mout"], w["ao"], w["out"])


def _ffn_kernel(prev_ref, x_ref, next_ref, mod_ref, n2w_ref, wup_ref, cw_ref, cb_ref, wdn_ref, o_ref,
                *, nc, nt, j0):
    j = pl.program_id(1) + j0
    has_prev = jnp.logical_and(j != 0, j != nc)
    has_next = jnp.logical_and(j != nc - 1, j != nt - 1)
    shift, scale, gate = mod_ref[0, 3:4, :], mod_ref[0, 4:5, :], mod_ref[0, 5:6, :]
    norm = lambda v: _modulated_rmsnorm(v, n2w_ref[...], shift, scale)
    x = x_ref[0]
    hcat = jnp.concatenate([jnp.where(has_prev, norm(prev_ref[0]), 0.0), norm(x),
                            jnp.where(has_next, norm(next_ref[0]), 0.0)], axis=0).astype(BF16)
    rows = TM + 2 * FFN_HALO

    def conv3(u, lo):
        w = cw_ref[:, lo:lo + FF_CHUNK]
        inner = lambda v: v[FFN_HALO:FFN_HALO + TM]
        return (w[0:1] * inner(pltpu.roll(u, 1, 0)) + w[1:2] * inner(u)
                + w[2:3] * inner(pltpu.roll(u, rows - 1, 0)) + cb_ref[:, lo:lo + FF_CHUNK])

    def up(lo):
        return _dot(hcat, wup_ref[:, lo:lo + FF_CHUNK]), _dot(hcat, wup_ref[:, D_FF + lo:D_FF + lo + FF_CHUNK])

    acc = None
    los = list(range(0, D_FF, FF_CHUNK))
    nxt = up(los[0])
    prev_act = None
    for i, lo in enumerate(los):
        ua, ug = nxt
        if i + 1 < len(los):
            nxt = up(los[i + 1])
        yg = conv3(ug, D_FF + lo)
        act = (conv3(ua, lo) * (yg * _sigmoid(yg))).astype(BF16)
        if prev_act is not None:
            part = _dot(prev_act, wdn_ref[los[i - 1]:los[i - 1] + FF_CHUNK, :])
            acc = part if acc is None else acc + part
        prev_act = act
    acc = acc + _dot(prev_act, wdn_ref[los[-1]:los[-1] + FF_CHUNK, :])
    o_ref[0] = x + gate * acc


def _conv_ffn(x1, modsel, n2w, w, nt, nc, j0):
    b, rows, d = x1.shape
    hpt = TM // FFN_HALO
    last = rows // FFN_HALO - 1
    kern = functools.partial(_ffn_kernel, nc=nc, nt=nt, j0=j0)
    return pl.pallas_call(
        kern,
        grid=(b, nt - j0),
        in_specs=[
            pl.BlockSpec((1, FFN_HALO, d), lambda bi, j: (bi, jnp.maximum(j * hpt - 1, 0), 0)),
            pl.BlockSpec((1, TM, d), lambda bi, j: (bi, j, 0)),
            pl.BlockSpec((1, FFN_HALO, d), lambda bi, j: (bi, jnp.minimum((j + 1) * hpt, last), 0)),
            pl.BlockSpec((1, 6, d), lambda bi, j: (bi * 2 + (j + j0 >= nc).astype(jnp.int32), 0, 0)),
            _resident((1, d)), _resident(w["up"].shape), _resident(w["fcw"].shape),
            _resident(w["fcb"].shape), _resident(w["down"].shape)],
        out_specs=pl.BlockSpec((1, TM, d), lambda bi, j: (bi, j, 0)),
        out_shape=jax.ShapeDtypeStruct((b, rows, d), F32),
        compiler_params=_params(("parallel", "parallel")),
        name="conv_ffn",
    )(x1, x1, x1, modsel, n2w, w["up"], w["fcw"], w["fcb"], w["down"])


def _rope_tables(n_ctx, seq):
    rows = seq // GRID_W
    row = jnp.broadcast_to(jnp.arange(rows, dtype=F32)[:, None], (rows, GRID_W)).reshape(-1)
    col = jnp.broadcast_to(jnp.arange(GRID_W, dtype=F32)[None, :], (rows, GRID_W)).reshape(-1)
    n_freq = ATTN_HEAD_DIM // 4
    inv_freq = ROPE_THETA ** (-jnp.arange(n_freq, dtype=F32) / n_freq)
    ang = jnp.concatenate([row[:, None] * inv_freq, col[:, None] * inv_freq], axis=-1)
    cos, sin = jnp.cos(ang), jnp.sin(ang)
    reps = LANES // ATTN_HEAD_DIM
    cos_t = jnp.tile(jnp.concatenate([cos, cos], axis=-1), (1, reps))
    sin_t = jnp.tile(jnp.concatenate([-sin, sin], axis=-1), (1, reps))
    cos_t = jnp.concatenate([jnp.ones((n_ctx, LANES), F32), cos_t], axis=0)
    sin_t = jnp.concatenate([jnp.zeros((n_ctx, LANES), F32), sin_t], axis=0)
    return cos_t, sin_t


def _layer_weights(l, d, w_in, branch_gate_b, mlstm_gate_b, q_norm_w, k_norm_w, w_conv_out, w_mlstm_out,
                   w_attn_out, w_out, ffn_w_up, ffn_conv_w, ffn_conv_b, ffn_w_down):
    sizes = (2 * CONV_DIM, 3 * MLSTM_DIM, MLSTM_DIM, 4 * MLSTM_HEADS, ATTN_DIM, KV_DIM, KV_DIM, N_BRANCH * d)
    offs = [int(o) for o in np.cumsum(sizes)[:-1]]
    cu, mqkv, mo, mg, aq, ak, av, bg = jnp.split(w_in[l].astype(BF16), offs, axis=-1)
    hd = ATTN_HEAD_DIM
    pad = jnp.zeros((d, LANES - hd), BF16)
    ak_dup = jnp.concatenate([ak[:, i * hd:(i + 1) * hd] for i in range(ATTN_KV_HEADS) for _ in range(LANES // hd)], axis=-1)
    av_ext = jnp.concatenate([p for i in range(ATTN_KV_HEADS) for p in (av[:, i * hd:(i + 1) * hd], pad)], axis=-1)
    wao = w_attn_out[l].astype(BF16).reshape(ATTN_HEADS, hd, d)
    wao = jnp.concatenate([wao, jnp.zeros((ATTN_HEADS, LANES - hd, d), BF16)], axis=1).reshape(ATTN_HEADS * LANES, d)
    return {
        "cu": cu, "mqkv": mqkv, "mo": mo, "mgt": mg.T, "mgbt": mlstm_gate_b[l].reshape(-1, 1),
        "aq": aq, "ak": ak_dup, "av": av_ext, "bg": bg, "bgb": branch_gate_b[l].reshape(1, -1),
        "qnw": jnp.tile(q_norm_w[l], ATTN_HEADS).reshape(1, -1),
        "knw": jnp.tile(k_norm_w[l], ATTN_KV_HEADS * LANES // hd).reshape(1, -1),
        "co": w_conv_out[l].astype(BF16), "mout": w_mlstm_out[l].astype(BF16), "ao": wao,
        "out": w_out[l].astype(BF16),
        "up": ffn_w_up[l].astype(BF16), "fcw": ffn_conv_w[l], "fcb": ffn_conv_b[l].reshape(1, -1),
        "down": ffn_w_down[l].astype(BF16),
    }


def kernel(x, c, ctx, c_ctx, ada_w, ada_b, norm1_w, norm2_w, w_in, branch_gate_b, conv_dw_w, conv_dw_b, conv_ln_w, conv_ln_b, w_conv_out, mlstm_gate_b, mlstm_norm_w, w_mlstm_out, q_norm_w, k_norm_w, w_attn_out, w_out, ffn_w_up, ffn_conv_w, ffn_conv_b, ffn_w_down):
    b, seq, d = x.shape
    n_ctx = ctx.shape[1]
    depth = ada_w.shape[0]
    assert seq % TM == 0 and n_ctx % TM == 0 and seq % GRID_W == 0
    nt, nc = (n_ctx + seq) // TM, n_ctx // TM

    pad_rows = -(b + 1) % 8
    cvecs = jnp.concatenate([c, c_ctx[None, :], jnp.zeros((pad_rows, d), F32)], axis=0)
    mods = _ada(cvecs, ada_w, ada_b).reshape(depth, -1, 6, d)
    modsel = jnp.stack([jnp.broadcast_to(mods[:, b:b + 1], (depth, b, 6, d)), mods[:, :b]], axis=2)
    modsel = modsel.reshape(depth, 2 * b, 6, d)

    cos_t, sin_t = _rope_tables(n_ctx, seq)
    xc, xl = ctx, x
    for l in range(depth):
        last = l == depth - 1
        j0 = nc if last else 0
        w = _layer_weights(l, d, w_in, branch_gate_b, mlstm_gate_b, q_norm_w, k_norm_w, w_conv_out,
                           w_mlstm_out, w_attn_out, w_out, ffn_w_up, ffn_conv_w, ffn_conv_b, ffn_w_down)
        glu, mqkv, sigo, grow, gwf, gwb, q, k, v, bgates = _inproj(
            xc, xl, modsel[l], norm1_w[l].reshape(1, d), w, cos_t, sin_t, nt, nc)
        hf, hb = _mlstm(mqkv, grow, gwf, gwb, nt, nc)
        attn_l, attn_c = _attention(q, k, v, n_ctx, want_ctx=not last)
        x1 = _merge(xc, xl, modsel[l], glu, (conv_dw_w[l], conv_dw_b[l], conv_ln_w[l], conv_ln_b[l]), hf, hb, sigo, attn_l if last else attn_c, attn_l, bgates,
                    mlstm_norm_w[l].reshape(1, -1), w, nt, nc, j0)
        xc = xl = _conv_ffn(x1, modsel[l], norm2_w[l].reshape(1, d), w, nt, nc, j0)
    return xl
```
